```python
import math
import jax, jax.numpy as jnp
from jax import lax
import numpy as np

D_MODEL = 1024
BATCH = 2
SEQ = 8192
DEPTH = 2
DEC_BATCH = 32
DEC_SEQ = 4
PAST_LEN = 16384
PAGE_SIZE = 128

N_A_LAYERS = DEPTH // 2
N_B_LAYERS = DEPTH - N_A_LAYERS
SSM_GROUP = 16
SSM_GROUPS = D_MODEL // SSM_GROUP
SSM_STATE = 64
DT_MIN = 1e-3
DT_MAX = 1e-1
HEAD_DIM = 64
HEADS_PER_BRANCH = D_MODEL // HEAD_DIM
KV_HEADS_PER_BRANCH = 4
Q_PER_KV = HEADS_PER_BRANCH // KV_HEADS_PER_BRANCH
BRANCHES = ((128, 1), (512, 4), (2048, 16))
N_BRANCH = len(BRANCHES)
Q_WIDTH = N_BRANCH * HEADS_PER_BRANCH * HEAD_DIM
KV_WIDTH = N_BRANCH * 2 * KV_HEADS_PER_BRANCH * HEAD_DIM
ATTN_OUT = HEADS_PER_BRANCH * HEAD_DIM
D_FF = 4 * D_MODEL
ROPE_THETA = 10000.0
NORM_EPS = 1e-6
MAX_Q_BLOCK = 128

kernel_name = "yoco_s5_dilated_swa_decode_step"


def rmsnorm(x, g):
    xf = x.astype(jnp.float32)
    y = xf * lax.rsqrt(jnp.mean(xf * xf, axis=-1, keepdims=True) + NORM_EPS)
    return (y * g.astype(jnp.float32)).astype(x.dtype)


def rope(x, pos):
    half = HEAD_DIM // 2
    inv = ROPE_THETA ** (-jnp.arange(half, dtype=jnp.float32) / half)
    ang = pos[:, None] * inv[None, :]
    shp = (ang.shape[0],) + (1,) * (x.ndim - 3) + (half,)
    cos = jnp.cos(ang).reshape(shp)
    sin = jnp.sin(ang).reshape(shp)
    xf = x.astype(jnp.float32)
    x1, x2 = xf[..., :half], xf[..., half:]
    return jnp.concatenate([x1 * cos - x2 * sin, x2 * cos + x1 * sin], axis=-1).astype(x.dtype)


def s5_mixer(u, h0_re, h0_im, a_re, a_im, log_dt, b_re, b_im, c_re, c_im, d_skip, w_glu):
    f32 = jnp.float32
    n, t, _ = u.shape
    uf = u.astype(f32)
    ug = uf.reshape(n, t, SSM_GROUPS, SSM_GROUP)
    dt = jnp.exp(log_dt.astype(f32))[:, None]
    ar, ai = a_re.astype(f32), a_im.astype(f32)
    mag = jnp.exp(ar * dt)
    lr, li = mag * jnp.cos(ai * dt), mag * jnp.sin(ai * dt)
    den = ar * ar + ai * ai
    nr, ni = lr - 1.0, li
    zr = (nr * ar + ni * ai) / den
    zi = (ni * ar - nr * ai) / den
    br, bi = b_re.astype(f32), b_im.astype(f32)
    bbr = zr[..., None] * br - zi[..., None] * bi
    bbi = zr[..., None] * bi + zi[..., None] * br
    xr = jnp.einsum('ntgc,gpc->ntgp', ug, bbr)
    xi = jnp.einsum('ntgc,gpc->ntgp', ug, bbi)
    if h0_re is not None:
        h0r, h0i = h0_re.astype(f32), h0_im.astype(f32)
        xr = xr.at[:, 0].add(lr * h0r - li * h0i)
        xi = xi.at[:, 0].add(lr * h0i + li * h0r)
    a_r = jnp.broadcast_to(lr, xr.shape)
    a_i = jnp.broadcast_to(li, xr.shape)

    def combine(e1, e2):
        a1r, a1i, b1r, b1i = e1
        a2r, a2i, b2r, b2i = e2
        return (a2r * a1r - a2i * a1i, a2r * a1i + a2i * a1r,
                a2r * b1r - a2i * b1i + b2r, a2r * b1i + a2i * b1r + b2i)

    _, _, hr, hi = lax.associative_scan(combine, (a_r, a_i, xr, xi), axis=1)
    y = (jnp.einsum('ntgp,gcp->ntgc', hr, c_re.astype(f32))
         - jnp.einsum('ntgp,gcp->ntgc', hi, c_im.astype(f32)))
    y = y.reshape(n, t, D_MODEL) + d_skip.astype(f32) * uf
    y = jax.nn.gelu(y).astype(u.dtype)
    z = y @ w_glu
    out = z[..., :D_MODEL] * jax.nn.sigmoid(z[..., D_MODEL:])
    return out, hr[:, -1], hi[:, -1]


def sqrelu_mlp(x, w_up, w_down):
    return jnp.square(jax.nn.relu(x @ w_up)) @ w_down


def shared_kv(h, pos, kv_norm, w_kv):
    n, t, _ = h.shape
    kv = (rmsnorm(h, kv_norm) @ w_kv).reshape(n, t, N_BRANCH, 2, KV_HEADS_PER_BRANCH, HEAD_DIM)
    k = rope(kv[:, :, :, 0], pos)
    kvs = jnp.stack([k, kv[:, :, :, 1]], axis=3)
    return [kvs[:, :, g] for g in range(N_BRANCH)]


def dilated_prompt(q, kv, window, dil):
    b, s = q.shape[:2]
    m = window // dil
    L = s // dil
    bq = math.gcd(L, MAX_Q_BLOCK)
    nb = L // bq
    kw = bq + m
    qs = q.reshape(b, L, dil, KV_HEADS_PER_BRANCH, Q_PER_KV, HEAD_DIM).transpose(0, 2, 1, 3, 4, 5)
    qs = qs.reshape(b, dil, nb, bq, KV_HEADS_PER_BRANCH, Q_PER_KV, HEAD_DIM)
    kvs = kv.reshape(b, L, dil, 2, KV_HEADS_PER_BRANCH, HEAD_DIM).transpose(0, 2, 1, 3, 4, 5)
    kvs = jnp.pad(kvs, ((0, 0), (0, 0), (m, 0), (0, 0), (0, 0), (0, 0)))
    kidx = (jnp.arange(nb) * bq)[:, None] + jnp.arange(kw)[None, :]
    kb = kvs[:, :, kidx]
    qpos = (jnp.arange(nb) * bq)[:, None] + jnp.arange(bq)[None, :]
    kpos = kidx - m
    dist = qpos[:, :, None] - kpos[:, None, :]
    mask = (dist >= 0) & (dist <= m) & (kpos[:, None, :] >= 0)
    sc = jnp.einsum('bwnqgre,bwnjge->bwngrqj', qs, kb[:, :, :, :, 0],
                    preferred_element_type=jnp.float32) * (HEAD_DIM ** -0.5)
    sc = jnp.where(mask[None, None, :, None, None], sc, -jnp.inf)
    lse = jax.nn.logsumexp(sc, axis=-1)
    p = jnp.exp(sc - lse[..., None])
    o = jnp.einsum('bwngrqj,bwnjge->bwnqgre', p.astype(kv.dtype), kb[:, :, :, :, 1],
                   preferred_element_type=jnp.float32)
    o = o.reshape(b, dil, L, KV_HEADS_PER_BRANCH, Q_PER_KV, HEAD_DIM).transpose(0, 2, 1, 3, 4, 5)
    o = o.reshape(b, s, HEADS_PER_BRANCH, HEAD_DIM)
    lse = lse.transpose(0, 1, 2, 5, 3, 4).reshape(b, dil, L, KV_HEADS_PER_BRANCH, Q_PER_KV)
    lse = lse.transpose(0, 2, 1, 3, 4).reshape(b, s, HEADS_PER_BRANCH)
    return o, lse


def dilated_sample(q, kv_new, kv_past, window, dil):
    n, t = q.shape[:2]
    wp = kv_past.shape[1]
    m = window // dil
    kv_all = jnp.concatenate([kv_past.astype(kv_new.dtype), kv_new], axis=1)
    idx = wp + jnp.arange(t)[:, None] - dil * jnp.arange(m + 1)[None, :]
    valid = idx >= 0
    kg = kv_all[:, jnp.clip(idx, 0, None)]
    qh = q.reshape(n, t, KV_HEADS_PER_BRANCH, Q_PER_KV, HEAD_DIM)
    sc = jnp.einsum('ntgre,ntjge->ntgrj', qh, kg[:, :, :, 0],
                    preferred_element_type=jnp.float32) * (HEAD_DIM ** -0.5)
    sc = jnp.where(valid[None, :, None, None, :], sc, -jnp.inf)
    lse = jax.nn.logsumexp(sc, axis=-1)
    p = jnp.exp(sc - lse[..., None])
    o = jnp.einsum('ntgrj,ntjge->ntgre', p.astype(kv_new.dtype), kg[:, :, :, 1],
                   preferred_element_type=jnp.float32)
    return o.reshape(n, t, HEADS_PER_BRANCH, HEAD_DIM), lse.reshape(n, t, HEADS_PER_BRANCH)


def dilated_mixer(h, pos, kv_branches, kv_past, w_q, w_o):
    n, t, _ = h.shape
    q = rope((h @ w_q).reshape(n, t, N_BRANCH, HEADS_PER_BRANCH, HEAD_DIM), pos)
    outs, lses = [], []
    for g, (win, dil) in enumerate(BRANCHES):
        if kv_past is None:
            o, l = dilated_prompt(q[:, :, g], kv_branches[g], win, dil)
        else:
            o, l = dilated_sample(q[:, :, g], kv_branches[g], kv_past[g], win, dil)
        outs.append(o)
        lses.append(l)
    wts = jax.nn.softmax(jnp.stack(lses, axis=0), axis=0)
    o = jnp.einsum('gnth,gnthe->nthe', wts, jnp.stack(outs, axis=0))
    return o.reshape(n, t, ATTN_OUT).astype(h.dtype) @ w_o


def run_trunk(x, pos, h0_re, h0_im, kv_past, norm_mix, norm_ffn, ssm_a_re, ssm_a_im, ssm_log_dt,
              ssm_b_re, ssm_b_im, ssm_c_re, ssm_c_im, ssm_d, w_glu, kv_norm, w_kv, w_q, w_o,
              w_up, w_down, final_norm):
    h = x
    fin_re, fin_im = [], []
    kv = None
    for i in range(DEPTH):
        hn = rmsnorm(h, norm_mix[i])
        if i < N_A_LAYERS:
            a = i
            mix, hr, hi = s5_mixer(hn, None if h0_re is None else h0_re[a],
                                   None if h0_im is None else h0_im[a],
                                   ssm_a_re[a], ssm_a_im[a], ssm_log_dt[a], ssm_b_re[a], ssm_b_im[a],
                                   ssm_c_re[a], ssm_c_im[a], ssm_d[a], w_glu[a])
            fin_re.append(hr)
            fin_im.append(hi)
        else:
            if kv is None:
                kv = shared_kv(h, pos, kv_norm, w_kv)
            j = i - N_A_LAYERS
            mix = dilated_mixer(hn, pos, kv, kv_past, w_q[j], w_o[j])
        h = h + mix
        h = h + sqrelu_mlp(rmsnorm(h, norm_ffn[i]), w_up[i], w_down[i])
    return rmsnorm(h, final_norm), kv, jnp.stack(fin_re, axis=0), jnp.stack(fin_im, axis=0)


def setup_inputs(seed: int = 0) -> dict:
    key = jax.random.key(seed)
    ks = iter(jax.random.split(key, 40))
    f32 = jnp.float32

    def nrm(shape, scale):
        return jax.random.normal(next(ks), shape, f32) * scale

    kvb = lambda w: (DEC_BATCH, min(w, PAST_LEN), 2, KV_HEADS_PER_BRANCH, HEAD_DIM)
    sshape = (N_A_LAYERS, DEC_BATCH, SSM_GROUPS, SSM_STATE)
    return {
        "x_prompt": nrm((BATCH, SEQ, D_MODEL), 1.0),
        "x_sample": nrm((DEC_BATCH, DEC_SEQ, D_MODEL), 1.0),
        "cache_kv_w128": nrm(kvb(BRANCHES[0][0]), 1.0),
        "cache_kv_w512": nrm(kvb(BRANCHES[1][0]), 1.0),
        "cache_kv_w2048": nrm(kvb(BRANCHES[2][0]), 1.0),
        "state_ssm_re": nrm(sshape, 0.1),
        "state_ssm_im": nrm(sshape, 0.1),
        "norm_mix": 1.0 + nrm((DEPTH, D_MODEL), 0.01),
        "norm_ffn": 1.0 + nrm((DEPTH, D_MODEL), 0.01),
        "ssm_a_re": -0.5 + nrm((N_A_LAYERS, SSM_GROUPS, SSM_STATE), 0.01),
        "ssm_a_im": math.pi * jnp.arange(SSM_STATE, dtype=f32) + nrm((N_A_LAYERS, SSM_GROUPS, SSM_STATE), 0.01),
        "ssm_log_dt": jax.random.uniform(next(ks), (N_A_LAYERS, SSM_GROUPS), f32,
                                         math.log(DT_MIN), math.log(DT_MAX)),
        "ssm_b_re": nrm((N_A_LAYERS, SSM_GROUPS, SSM_STATE, SSM_GROUP), (2 * SSM_GROUP) ** -0.5),
        "ssm_b_im": nrm((N_A_LAYERS, SSM_GROUPS, SSM_STATE, SSM_GROUP), (2 * SSM_GROUP) ** -0.5),
        "ssm_c_re": nrm((N_A_LAYERS, SSM_GROUPS, SSM_GROUP, SSM_STATE), SSM_STATE ** -0.5),
        "ssm_c_im": nrm((N_A_LAYERS, SSM_GROUPS, SSM_GROUP, SSM_STATE), SSM_STATE ** -0.5),
        "ssm_d": nrm((N_A_LAYERS, D_MODEL), 1.0),
        "w_glu": nrm((N_A_LAYERS, D_MODEL, 2 * D_MODEL), D_MODEL ** -0.5),
        "kv_norm": 1.0 + nrm((D_MODEL,), 0.01),
        "w_kv": nrm((D_MODEL, KV_WIDTH), D_MODEL ** -0.5),
        "w_q": nrm((N_B_LAYERS, D_MODEL, Q_WIDTH), D_MODEL ** -0.5),
        "w_o": nrm((N_B_LAYERS, ATTN_OUT, D_MODEL), ATTN_OUT ** -0.5),
        "w_up": nrm((DEPTH, D_MODEL, D_FF), D_MODEL ** -0.5),
        "w_down": nrm((DEPTH, D_FF, D_MODEL), D_FF ** -0.5),
        "final_norm": 1.0 + nrm((D_MODEL,), 0.01),
    }


def reference(x_prompt, x_sample, cache_kv_w128, cache_kv_w512, cache_kv_w2048, state_ssm_re, state_ssm_im,
              norm_mix, norm_ffn, ssm_a_re, ssm_a_im, ssm_log_dt, ssm_b_re, ssm_b_im, ssm_c_re, ssm_c_im,
              ssm_d, w_glu, kv_norm, w_kv, w_q, w_o, w_up, w_down, final_norm):
    seq = x_prompt.shape[1]
    dec_seq = x_sample.shape[1]
    pos_p = jnp.arange(seq, dtype=jnp.float32)
    pos_s = PAST_LEN + jnp.arange(dec_seq, dtype=jnp.float32)
    y_prompt, kv_p, re_p, im_p = run_trunk(
        x_prompt, pos_p, None, None, None, norm_mix, norm_ffn, ssm_a_re, ssm_a_im, ssm_log_dt,
        ssm_b_re, ssm_b_im, ssm_c_re, ssm_c_im, ssm_d, w_glu, kv_norm, w_kv, w_q, w_o, w_up, w_down, final_norm)
    y_sample, kv_s, re_s, im_s = run_trunk(
        x_sample, pos_s, state_ssm_re, state_ssm_im, (cache_kv_w128, cache_kv_w512, cache_kv_w2048),
        norm_mix, norm_ffn, ssm_a_re, ssm_a_im, ssm_log_dt, ssm_b_re, ssm_b_im, ssm_c_re, ssm_c_im,
        ssm_d, w_glu, kv_norm, w_kv, w_q, w_o, w_up, w_down, final_norm)
    w0 = min(BRANCHES[0][0], seq)
    w1 = min(BRANCHES[1][0], seq)
    w2 = min(BRANCHES[2][0], seq)
    return (y_prompt, y_sample,
            kv_p[0][:, seq - w0:], kv_p[1][:, seq - w1:], kv_p[2][:, seq - w2:],
            kv_s[0], kv_s[1], kv_s[2],
            re_p, im_p, re_s, im_s)
```

```python
import functools
import math

import jax
import jax.numpy as jnp
from jax import lax
from jax.experimental import pallas as pl
from jax.experimental.pallas import tpu as pltpu

F32 = jnp.float32
BF16 = jnp.bfloat16

D_MODEL = 1024
SSM_GROUP = 16
SSM_GROUPS = D_MODEL // SSM_GROUP
SSM_STATE = 64
N_PAIRS = SSM_GROUPS // 2
CHUNK = 16
CW = CHUNK * SSM_GROUP
HEAD_DIM = 64
N_HEADS = 16
KV_HEADS = 4
BRANCHES = ((128, 1), (512, 4), (2048, 16))
N_BRANCH = 3
BAND = 128
Q_WIDTH = N_BRANCH * N_HEADS * HEAD_DIM
KV_BRANCH_WIDTH = 2 * KV_HEADS * HEAD_DIM
KV_WIDTH = N_BRANCH * KV_BRANCH_WIDTH
D_FF = 4 * D_MODEL
FF_CHUNK = 1024
ROPE_THETA = 10000.0
NORM_EPS = 1e-6
PAST_LEN = 16384
LANES = 128
NEG_BIG = -1e30
VMEM_LIMIT = 56 * 1024 * 1024
HIGHEST = lax.Precision.HIGHEST


def _cparams(sem):
    return pltpu.CompilerParams(dimension_semantics=sem, vmem_limit_bytes=VMEM_LIMIT)


def _const_spec(shape):
    nd = len(shape)
    return pl.BlockSpec(shape, lambda *_: (0,) * nd, pipeline_mode=pl.Buffered(1))


def _rms_unit(x):
    return x * lax.rsqrt(jnp.mean(x * x, axis=-1, keepdims=True) + NORM_EPS)


def _gelu_tanh(x):
    c = math.sqrt(2.0 / math.pi)
    return x * (0.5 * (1.0 + jnp.tanh(c * (x + 0.044715 * (x * x * x)))))


def _dot(a, b):
    return jnp.dot(a, b, preferred_element_type=F32)


def _dot_nt(a, b):
    return lax.dot_general(a, b, (((1,), (1,)), ((), ())), preferred_element_type=F32)


def _prenorm_kernel(x_ref, g_ref, o_ref):
    o_ref[...] = (_rms_unit(x_ref[...]) * g_ref[...]).astype(BF16)


def _prenorm(x, g, tm):
    m = x.shape[0]
    return pl.pallas_call(
        _prenorm_kernel,
        grid=(m // tm,),
        in_specs=[pl.BlockSpec((tm, D_MODEL), lambda i: (i, 0)), _const_spec((1, D_MODEL))],
        out_specs=pl.BlockSpec((tm, D_MODEL), lambda i: (i, 0)),
        out_shape=jax.ShapeDtypeStruct((m, D_MODEL), BF16),
        compiler_params=_cparams(("parallel",)),
        name="prenorm",
    )(x, g)


def _s5_build_kernel(arow_ref, airow_ref, ldtrow_ref, acol_ref, aicol_ref, ldtcol_ref,
                     brt_ref, bit_ref, crt_ref, cit_ref,
                     toep_ref, wstr_ref, wsti_ref, wstr_s_ref, wsti_s_ref, woutr_ref, wouti_ref, dec_ref,
                     *, sample_tokens):
    ar, ai = arow_ref[...], airow_ref[...]
    dt = jnp.exp(ldtrow_ref[...])
    mag = jnp.exp(ar * dt)
    lr, li = mag * jnp.cos(ai * dt), mag * jnp.sin(ai * dt)
    den = ar * ar + ai * ai
    nr, ni = lr - 1.0, li
    zr = (nr * ar + ni * ai) / den
    zi = (ni * ar - nr * ai) / den
    br, bi = brt_ref[...], bit_ref[...]
    bxr = zr * br - zi * bi
    bxi = zr * bi + zi * br

    def power_rows(e):
        m = jnp.exp(e * ar * dt)
        return m * jnp.cos(e * ai * dt), m * jnp.sin(e * ai * dt)

    bxr_t = jnp.tile(bxr, (CHUNK, 1))
    bxi_t = jnp.tile(bxi, (CHUNK, 1))
    tok = lax.broadcasted_iota(jnp.int32, (CW, SSM_STATE), 0) // SSM_GROUP

    def state_weights(n_tokens, wr_ref, wi_ref):
        e = n_tokens - 1 - tok
        live = e >= 0
        pr, pi = power_rows(jnp.maximum(e, 0).astype(F32))
        wr_ref[...] = jnp.where(live, bxr_t * pr - bxi_t * pi, 0.0)
        wi_ref[...] = jnp.where(live, bxr_t * pi + bxi_t * pr, 0.0)

    state_weights(CHUNK, wstr_ref, wsti_ref)
    state_weights(sample_tokens, wstr_s_ref, wsti_s_ref)

    dr, di = power_rows(float(CHUNK))
    dr_s, di_s = power_rows(float(sample_tokens))
    for row, val in enumerate((dr, di, dr_s, di_s)):
        dec_ref[row:row + 1, :] = val

    arc, aic = acol_ref[...], aicol_ref[...]
    dtc = jnp.exp(ldtcol_ref[...])
    crt, cit = crt_ref[...], cit_ref[...]
    lag = (lax.broadcasted_iota(jnp.int32, (SSM_STATE, CW), 1) // SSM_GROUP).astype(F32)

    def psi(e):
        m = jnp.exp(e * arc * dtc)
        ur, ui = m * jnp.cos(e * aic * dtc), m * jnp.sin(e * aic * dtc)
        return crt * ur - cit * ui, -(crt * ui) - cit * ur

    p0r, p0i = psi(lag)
    wr, wi = psi(lag + 1.0)
    woutr_ref[...] = wr
    wouti_ref[...] = wi

    taps = (jnp.dot(bxr, p0r, precision=HIGHEST, preferred_element_type=F32)
            + jnp.dot(bxi, p0i, precision=HIGHEST, preferred_element_type=F32))
    lane = lax.broadcasted_iota(jnp.int32, (SSM_GROUP, CW), 1)
    for i in range(CHUNK):
        blk = taps if i == 0 else jnp.where(lane >= SSM_GROUP * i, pltpu.roll(taps, SSM_GROUP * i, 1), 0.0)
        toep_ref[SSM_GROUP * i:SSM_GROUP * (i + 1), :] = blk.astype(BF16)


def _s5_build(a_re, a_im, log_dt, b_re, b_im, c_re, c_im, sample_tokens):
    g, p, c = SSM_GROUPS, SSM_STATE, SSM_GROUP
    ldt = jnp.broadcast_to(log_dt[:, None], (g, p))
    rows = [v.reshape(g, 1, p) for v in (a_re, a_im, ldt)]
    cols = [v.reshape(g, p, 1) for v in (a_re, a_im, ldt)]
    brt, bit = (jnp.swapaxes(v, 1, 2) for v in (b_re, b_im))
    crt, cit = (jnp.tile(jnp.swapaxes(v, 1, 2), (1, 1, CHUNK)) for v in (c_re, c_im))

    def gspec(*tail):
        return pl.BlockSpec((None,) + tail, lambda i: (i,) + (0,) * len(tail))

    out_shapes = [
        jax.ShapeDtypeStruct((g, CW, CW), BF16),
        jax.ShapeDtypeStruct((g, CW, p), F32), jax.ShapeDtypeStruct((g, CW, p), F32),
        jax.ShapeDtypeStruct((g, CW, p), F32), jax.ShapeDtypeStruct((g, CW, p), F32),
        jax.ShapeDtypeStruct((g, p, CW), F32), jax.ShapeDtypeStruct((g, p, CW), F32),
        jax.ShapeDtypeStruct((g, 4, p), F32),
    ]
    return pl.pallas_call(
        functools.partial(_s5_build_kernel, sample_tokens=sample_tokens),
        grid=(g,),
        in_specs=[gspec(1, p)] * 3 + [gspec(p, 1)] * 3 + [gspec(c, p)] * 2 + [gspec(p, CW)] * 2,
        out_specs=[gspec(CW, CW)] + [gspec(CW, p)] * 4 + [gspec(p, CW)] * 2 + [gspec(4, p)],
        out_shape=out_shapes,
        compiler_params=_cparams(("parallel",)),
        name="s5_build",
    )(*rows, *cols, brt, bit, crt, cit)


def _pair_state_weights(wr, wi):
    wr = wr.reshape(N_PAIRS, 2, CW, SSM_STATE)
    wi = wi.reshape(N_PAIRS, 2, CW, SSM_STATE)
    z = jnp.zeros_like(wr[:, 0])
    top = jnp.concatenate([wr[:, 0], z, wi[:, 0], z], axis=-1)
    bot = jnp.concatenate([z, wr[:, 1], z, wi[:, 1]], axis=-1)
    return jnp.concatenate([top, bot], axis=1).astype(BF16)


def _pair_out_weights(wr, wi):
    wr = wr.reshape(N_PAIRS, 2, SSM_STATE, CW)
    wi = wi.reshape(N_PAIRS, 2, SSM_STATE, CW)
    z = jnp.zeros_like(wr[:, 0])
    rows = [jnp.concatenate([wr[:, 0], z], -1), jnp.concatenate([z, wr[:, 1]], -1),
            jnp.concatenate([wi[:, 0], z], -1), jnp.concatenate([z, wi[:, 1]], -1)]
    return jnp.concatenate(rows, axis=1).astype(BF16)


def _s5_chunk_kernel(u_ref, toep_ref, wst_ref, wout_ref, dre_ref, dim_ref, h0_ref,
                     y_ref, hf_ref, s_ref, hp_ref, *, n_seq, n_chunks):
    half = 2 * SSM_STATE
    u = u_ref[...]
    s_ref[...] = _dot(u, wst_ref[...])
    dr, di = dre_ref[...], dim_ref[...]

    def step(hr, hi, s):
        return dr * hr - di * hi + s[:, :half], dr * hi + di * hr + s[:, half:]

    if n_chunks == 1:
        h0 = h0_ref[...]
        hp_ref[...] = h0
        hr, hi = step(h0[:, :half], h0[:, half:], s_ref[...])
        hf_ref[...] = jnp.concatenate([hr, hi], axis=1)
    else:
        def body(c, carry):
            out = []
            for b in range(n_seq):
                hr, hi = carry[b]
                row = b * n_chunks + c
                hp_ref[pl.ds(row, 1), :] = jnp.concatenate([hr, hi], axis=1)
                out.append(step(hr, hi, s_ref[pl.ds(row, 1), :]))
            return tuple(out)

        init = tuple((h0_ref[b:b + 1, :half], h0_ref[b:b + 1, half:]) for b in range(n_seq))
        fin = lax.fori_loop(0, n_chunks, body, init, unroll=8)
        for b in range(n_seq):
            hf_ref[b:b + 1, :] = jnp.concatenate(fin[b], axis=1)

    y_intra = jnp.concatenate([_dot(u[:, :CW], toep_ref[0]), _dot(u[:, CW:], toep_ref[1])], axis=1)
    y_ref[...] = y_intra + _dot(hp_ref[...].astype(BF16), wout_ref[...])


def _s5_chunks(u_pairs, toep, wst, wout, dre, dim, h0, n_seq, n_chunks):
    r = n_seq * n_chunks

    def pspec(*tail):
        return pl.BlockSpec((None,) + tail, lambda i: (i,) + (0,) * len(tail))

    return pl.pallas_call(
        functools.partial(_s5_chunk_kernel, n_seq=n_seq, n_chunks=n_chunks),
        grid=(N_PAIRS,),
        in_specs=[pspec(r, 2 * CW), pl.BlockSpec((2, CW, CW), lambda i: (i, 0, 0)),
                  pspec(2 * CW, 4 * SSM_STATE), pspec(4 * SSM_STATE, 2 * CW),
                  pspec(1, 2 * SSM_STATE), pspec(1, 2 * SSM_STATE), pspec(n_seq, 4 * SSM_STATE)],
        out_specs=[pspec(r, 2 * CW), pspec(n_seq, 4 * SSM_STATE)],
        out_shape=[jax.ShapeDtypeStruct((N_PAIRS, r, 2 * CW), F32),
                   jax.ShapeDtypeStruct((N_PAIRS, n_seq, 4 * SSM_STATE), F32)],
        scratch_shapes=[pltpu.VMEM((r, 4 * SSM_STATE), F32), pltpu.VMEM((r, 4 * SSM_STATE), F32)],
        compiler_params=_cparams(("parallel",)),
        name="s5_chunks",
    )(u_pairs, toep, wst, wout, dre, dim, h0)


def _mlp_residual(h, g_ffn, wup_ref, wdn_ref):
    hn = (_rms_unit(h) * g_ffn).astype(BF16)
    acc = h
    for k in range(D_FF // FF_CHUNK):
        a = _dot(hn, wup_ref[:, k * FF_CHUNK:(k + 1) * FF_CHUNK])
        a = jnp.square(jnp.maximum(a, 0.0)).astype(BF16)
        acc = acc + _dot(a, wdn_ref[k * FF_CHUNK:(k + 1) * FF_CHUNK, :])
    return acc


def _s5_tail_kernel(x_ref, y_ref, gmix_ref, dskip_ref, wglu_ref, gffn_ref, wup_ref, wdn_ref, o_ref):
    x = x_ref[...]
    u = _rms_unit(x) * gmix_ref[...]
    v = _gelu_tanh(y_ref[...] + dskip_ref[...] * u).astype(BF16)
    z = _dot(v, wglu_ref[...])
    h = x + z[:, :D_MODEL] * jax.nn.sigmoid(z[:, D_MODEL:])
    o_ref[...] = _mlp_residual(h, gffn_ref[...], wup_ref, wdn_ref)


def _s5_tail(x, y, gmix, dskip, wglu, gffn, wup, wdn, tm):
    m = x.shape[0]
    row = pl.BlockSpec((tm, D_MODEL), lambda i: (i, 0))
    vec = _const_spec((1, D_MODEL))
    return pl.pallas_call(
        _s5_tail_kernel,
        grid=(m // tm,),
        in_specs=[row, row, vec, vec, _const_spec((D_MODEL, 2 * D_MODEL)), vec,
                  _const_spec((D_MODEL, D_FF)), _const_spec((D_FF, D_MODEL))],
        out_specs=row,
        out_shape=jax.ShapeDtypeStruct((m, D_MODEL), F32),
        compiler_params=_cparams(("parallel",)),
        name="s5_tail",
    )(x, y, gmix, dskip, wglu, gffn, wup, wdn)


def _attn_tail_kernel(h_ref, o0_ref, o1_ref, o2_ref, l0_ref, l1_ref, l2_ref, expand_ref,
                      wo_ref, gffn_ref, wup_ref, wdn_ref, gfin_ref, out_ref):
    lses = [l0_ref[...], l1_ref[...], l2_ref[...]]
    mx = jnp.maximum(jnp.maximum(lses[0], lses[1]), lses[2])
    ws = [jnp.exp(l - mx) for l in lses]
    tot = ws[0] + ws[1] + ws[2]
    mixed = None
    for w, o_ref in zip(ws, (o0_ref, o1_ref, o2_ref)):
        wfull = jnp.dot(w / tot, expand_ref[...], precision=HIGHEST, preferred_element_type=F32)
        term = wfull * o_ref[...]
        mixed = term if mixed is None else mixed + term
    h = h_ref[...] + _dot(mixed.astype(BF16), wo_ref[...])
    h = _mlp_residual(h, gffn_ref[...], wup_ref, wdn_ref)
    out_ref[...] = _rms_unit(h) * gfin_ref[...]


def _attn_tail(h, outs, lses, wo, gffn, wup, wdn, gfin, tm):
    m = h.shape[0]
    row = pl.BlockSpec((tm, D_MODEL), lambda i: (i, 0))
    lrow = pl.BlockSpec((tm, LANES), lambda i: (i, 0))
    vec = _const_spec((1, D_MODEL))
    head_of_lane = jnp.arange(D_MODEL, dtype=jnp.int32) // HEAD_DIM
    expand = (jnp.arange(LANES, dtype=jnp.int32)[:, None] == head_of_lane[None, :]).astype(F32)
    return pl.pallas_call(
        _attn_tail_kernel,
        grid=(m // tm,),
        in_specs=[row, row, row, row, lrow, lrow, lrow, _const_spec((LANES, D_MODEL)),
                  _const_spec((D_MODEL, D_MODEL)), vec, _const_spec((D_MODEL, D_FF)),
                  _const_spec((D_FF, D_MODEL)), vec],
        out_specs=row,
        out_shape=jax.ShapeDtypeStruct((m, D_MODEL), F32),
        compiler_params=_cparams(("parallel",)),
        name="attn_tail",
    )(h, *outs, *lses, expand, wo, gffn, wup, wdn, gfin)


def _qkv_kernel(h_ref, gq_ref, gkv_ref, wq_ref, wkv_ref, cos_ref, sin_ref, q_ref, kv_ref, kvb_ref):
    r = _rms_unit(h_ref[...])
    hq = (r * gq_ref[...]).astype(BF16)
    hk = (r * gkv_ref[...]).astype(BF16)
    cos, sin = cos_ref[...], sin_ref[...]
    tm = cos.shape[0]
    first = (lax.broadcasted_iota(jnp.int32, (tm, LANES), 1) % HEAD_DIM) < (HEAD_DIM // 2)

    def rope(blk):
        partner = jnp.where(first, pltpu.roll(blk, LANES - HEAD_DIM // 2, 1), pltpu.roll(blk, HEAD_DIM // 2, 1))
        return blk * cos + partner * sin

    step = KV_BRANCH_WIDTH
    for c in range(Q_WIDTH // step):
        qc = _dot(hq, wq_ref[:, c * step:(c + 1) * step])
        for j in range(step // LANES):
            q_ref[:, c * step + j * LANES:c * step + (j + 1) * LANES] = rope(qc[:, j * LANES:(j + 1) * LANES]).astype(BF16)
    k_cols = KV_HEADS * HEAD_DIM
    for g in range(N_BRANCH):
        kvc = _dot(hk, wkv_ref[:, g * step:(g + 1) * step])
        for j in range(step // LANES):
            blk = kvc[:, j * LANES:(j + 1) * LANES]
            if j * LANES < k_cols:
                blk = rope(blk)
            kv_ref[:, g * step + j * LANES:g * step + (j + 1) * LANES] = blk
            kvb_ref[:, g * step + j * LANES:g * step + (j + 1) * LANES] = blk.astype(BF16)


def _qkv(h, gq, gkv, wq, wkv, cos_t, sin_t, tm):
    m = h.shape[0]
    n_tab = cos_t.shape[0] // tm
    row = pl.BlockSpec((tm, D_MODEL), lambda i: (i, 0))
    tab = pl.BlockSpec((tm, LANES), lambda i: (i % n_tab, 0))
    vec = _const_spec((1, D_MODEL))
    return pl.pallas_call(
        _qkv_kernel,
        grid=(m // tm,),
        in_specs=[row, vec, vec, _const_spec((D_MODEL, Q_WIDTH)), _const_spec((D_MODEL, KV_WIDTH)), tab, tab],
        out_specs=[pl.BlockSpec((tm, Q_WIDTH), lambda i: (i, 0)), pl.BlockSpec((tm, KV_WIDTH), lambda i: (i, 0)),
                   pl.BlockSpec((tm, KV_WIDTH), lambda i: (i, 0))],
        out_shape=[jax.ShapeDtypeStruct((m, Q_WIDTH), BF16), jax.ShapeDtypeStruct((m, KV_WIDTH), F32),
                   jax.ShapeDtypeStruct((m, KV_WIDTH), BF16)],
        compiler_params=_cparams(("parallel",)),
        name="qkv_rope",
    )(h, gq, gkv, wq, wkv, cos_t, sin_t)


def _rope_tables(pos):
    half = HEAD_DIM // 2
    inv = ROPE_THETA ** (-jnp.arange(half, dtype=F32) / half)
    ang = pos[:, None] * inv[None, :]
    cos, sin = jnp.cos(ang), jnp.sin(ang)
    return jnp.concatenate([cos, cos, cos, cos], axis=1), jnp.concatenate([-sin, sin, -sin, sin], axis=1)


def _band_attn_kernel(q_ref, kvp_ref, kvc_ref, o_ref, lse_ref):
    i = pl.program_id(2)
    bq = BAND
    kv = jnp.concatenate([kvp_ref[...], kvc_ref[...]], axis=0)
    lane = lax.broadcasted_iota(jnp.int32, (1, LANES), 1)
    lo = lane < HEAD_DIM
    a = lax.broadcasted_iota(jnp.int32, (2 * bq, 2 * bq), 0) % bq
    j = lax.broadcasted_iota(jnp.int32, (2 * bq, 2 * bq), 1)
    valid = (j >= a) & (j <= a + bq) & ((i > 0) | (j >= bq))
    scale = HEAD_DIM ** -0.5
    k_cols = KV_HEADS * HEAD_DIM
    lse_tile = jnp.zeros((bq, LANES), F32)
    for pair in range(KV_HEADS // 2):
        k2 = kv[:, pair * LANES:(pair + 1) * LANES]
        v2 = kv[:, k_cols + pair * LANES:k_cols + (pair + 1) * LANES]
        k2r, v2r = pltpu.roll(k2, HEAD_DIM, 1), pltpu.roll(v2, HEAD_DIM, 1)
        for s in range(2):
            kd = jnp.where(lo, k2, k2r) if s == 0 else jnp.where(lo, k2r, k2)
            vd = jnp.where(lo, v2, v2r) if s == 0 else jnp.where(lo, v2r, v2)
            kvh = 2 * pair + s
            for cc in range(2):
                col = 2 * kvh + cc
                qc = q_ref[:, col * LANES:(col + 1) * LANES]
                zero = jnp.zeros_like(qc)
                q2 = jnp.concatenate([jnp.where(lo, qc, zero), jnp.where(lo, zero, qc)], axis=0)
                sc = jnp.where(valid, _dot_nt(q2, kd) * scale, NEG_BIG)
                mx = jnp.max(sc, axis=1, keepdims=True)
                p = jnp.exp(sc - mx)
                den = jnp.sum(p, axis=1, keepdims=True)
                pv = _dot(p.astype(BF16), vd) / den
                o_ref[:, col * LANES:(col + 1) * LANES] = jnp.where(lo, pv[:bq], pv[bq:])
                lse = mx + jnp.log(den)
                lse_tile = jnp.where(lane == 2 * col, lse[:bq], lse_tile)
                lse_tile = jnp.where(lane == 2 * col + 1, lse[bq:], lse_tile)
    lse_ref[...] = lse_tile


def _band_attn(q, kvb, batch, seq, g, dil):
    ls = seq // dil
    nb = ls // BAND
    qv = q.reshape(batch, ls, dil * Q_WIDTH)
    kvv = kvb.reshape(batch, ls, dil * KV_WIDTH)
    o, lse = pl.pallas_call(
        _band_attn_kernel,
        grid=(batch, dil, nb),
        in_specs=[
            pl.BlockSpec((None, BAND, D_MODEL), lambda b, r, i: (b, i, r * N_BRANCH + g)),
            pl.BlockSpec((None, BAND, KV_BRANCH_WIDTH), lambda b, r, i: (b, jnp.maximum(i - 1, 0), r * N_BRANCH + g)),
            pl.BlockSpec((None, BAND, KV_BRANCH_WIDTH), lambda b, r, i: (b, i, r * N_BRANCH + g)),
        ],
        out_specs=[pl.BlockSpec((None, BAND, D_MODEL), lambda b, r, i: (b, i, r)),
                   pl.BlockSpec((None, BAND, LANES), lambda b, r, i: (b, i, r))],
        out_shape=[jax.ShapeDtypeStruct((batch, ls, dil * D_MODEL), F32),
                   jax.ShapeDtypeStruct((batch, ls, dil * LANES), F32)],
        compiler_params=_cparams(("parallel", "parallel", "arbitrary")),
        name=f"band_attn_d{dil}",
    )(qv, kvv, kvv)
    return o.reshape(batch * seq, D_MODEL), lse.reshape(batch * seq, LANES)


def _sample_attn_kernel(qe_ref, kvn_ref, c0_ref, c1_ref, c2_ref, o_ref, lse_ref, *, n_tok):
    kw = KV_HEADS * HEAD_DIM
    hmask = (lax.broadcasted_iota(jnp.int32, (N_HEADS, kw), 0) // (N_HEADS // KV_HEADS)
             == lax.broadcasted_iota(jnp.int32, (N_HEADS, kw), 1) // HEAD_DIM)
    jc = lax.broadcasted_iota(jnp.int32, (N_HEADS, BAND), 1)
    n_new = kvn_ref.shape[0]
    jn = lax.broadcasted_iota(jnp.int32, (N_HEADS, n_new), 1)
    scale = HEAD_DIM ** -0.5
    for g, (cref, (_, dil)) in enumerate(zip((c0_ref, c1_ref, c2_ref), BRANCHES)):
        kn = kvn_ref[:, g * KV_BRANCH_WIDTH:g * KV_BRANCH_WIDTH + kw].astype(BF16)
        vn = kvn_ref[:, g * KV_BRANCH_WIDTH + kw:(g + 1) * KV_BRANCH_WIDTH].astype(BF16)
        for t in range(n_tok):
            off = 0 if dil == 1 else t * KV_BRANCH_WIDTH
            kc = cref[:, off:off + kw].astype(BF16)
            vc = cref[:, off + kw:off + KV_BRANCH_WIDTH].astype(BF16)
            r0 = (g * n_tok + t) * N_HEADS
            qe = qe_ref[r0:r0 + N_HEADS, :]
            qe = jnp.where(hmask, qe, jnp.zeros_like(qe))
            sc = _dot_nt(qe, kc) * scale
            sn = _dot_nt(qe, kn) * scale
            if dil == 1:
                sc = jnp.where(jc >= t, sc, NEG_BIG)
                sn = jnp.where(jn <= t, sn, NEG_BIG)
            else:
                sn = jnp.where(jn == t, sn, NEG_BIG)
            mx = jnp.maximum(jnp.max(sc, axis=1, keepdims=True), jnp.max(sn, axis=1, keepdims=True))
            pc, pn = jnp.exp(sc - mx), jnp.exp(sn - mx)
            den = jnp.sum(pc, axis=1, keepdims=True) + jnp.sum(pn, axis=1, keepdims=True)
            o = (_dot(pc.astype(BF16), vc) + _dot(pn.astype(BF16), vn)) / den
            o = jnp.where(hmask, o, 0.0)
            o_ref[r0:r0 + N_HEADS, :] = (o[:, 0:64] + o[:, 64:128]) + (o[:, 128:192] + o[:, 192:256])
            lse_ref[r0:r0 + N_HEADS, :] = jnp.broadcast_to(mx + jnp.log(den), (N_HEADS, LANES))


def _sample_attn(q, kv, caches, n_seq, n_tok):
    kw = KV_HEADS * HEAD_DIM
    qe = q.reshape(n_seq, n_tok, N_BRANCH, N_HEADS, HEAD_DIM).transpose(0, 2, 1, 3, 4)
    qe = jnp.tile(qe, (1, 1, 1, 1, KV_HEADS)).reshape(n_seq, N_BRANCH * n_tok * N_HEADS, kw)
    n_new = 16
    kvn = jnp.pad(kv.reshape(n_seq, n_tok, KV_WIDTH), ((0, 0), (0, n_new - n_tok), (0, 0)))
    views, specs = [], []
    for cache, (win, dil) in zip(caches, BRANCHES):
        assert cache.shape[1] == win and win // dil == BAND and (dil == 1 or n_tok <= dil)
        views.append(cache.reshape(n_seq, BAND, dil * KV_BRANCH_WIDTH))
        width = KV_BRANCH_WIDTH if dil == 1 else n_tok * KV_BRANCH_WIDTH
        specs.append(pl.BlockSpec((None, BAND, width), lambda n: (n, 0, 0)))
    rows = N_BRANCH * n_tok * N_HEADS
    o, lse = pl.pallas_call(
        functools.partial(_sample_attn_kernel, n_tok=n_tok),
        grid=(n_seq,),
        in_specs=[pl.BlockSpec((None, rows, kw), lambda n: (n, 0, 0)),
                  pl.BlockSpec((None, n_new, KV_WIDTH), lambda n: (n, 0, 0))] + specs,
        out_specs=[pl.BlockSpec((None, rows, HEAD_DIM), lambda n: (n, 0, 0)),
                   pl.BlockSpec((None, rows, LANES), lambda n: (n, 0, 0))],
        out_shape=[jax.ShapeDtypeStruct((n_seq, rows, HEAD_DIM), F32),
                   jax.ShapeDtypeStruct((n_seq, rows, LANES), F32)],
        compiler_params=_cparams(("parallel",)),
        name="sample_attn",
    )(qe, kvn, *views)
    o = o.reshape(n_seq, N_BRANCH, n_tok, D_MODEL).transpose(1, 0, 2, 3).reshape(N_BRANCH, n_seq * n_tok, D_MODEL)
    lse = lse[:, :, 0].reshape(n_seq, N_BRANCH, n_tok, N_HEADS).transpose(1, 0, 2, 3)
    lse = jnp.pad(lse.reshape(N_BRANCH, n_seq * n_tok, N_HEADS), ((0, 0), (0, 0), (0, LANES - N_HEADS)))
    return [o[g] for g in range(N_BRANCH)], [lse[g] for g in range(N_BRANCH)]


def _to_pairs(u, n_seq, n_tok):
    pad = (-n_tok) % CHUNK
    u = u.reshape(n_seq, n_tok, D_MODEL)
    if pad:
        u = jnp.pad(u, ((0, 0), (0, pad), (0, 0)))
    nc = (n_tok + pad) // CHUNK
    u = u.reshape(n_seq, nc, CHUNK, N_PAIRS, 2, SSM_GROUP).transpose(3, 0, 1, 4, 2, 5)
    return u.reshape(N_PAIRS, n_seq * nc, 2 * CW), nc


def _from_pairs(y, n_seq, n_tok, nc):
    y = y.reshape(N_PAIRS, n_seq, nc, 2, CHUNK, SSM_GROUP).transpose(1, 2, 4, 0, 3, 5)
    y = y.reshape(n_seq, nc * CHUNK, D_MODEL)[:, :n_tok]
    return y.reshape(n_seq * n_tok, D_MODEL)


def _state_to_pairs(h_re, h_im):
    n = h_re.shape[0]
    f = lambda h: h.reshape(n, N_PAIRS, 2 * SSM_STATE).transpose(1, 0, 2)
    return jnp.concatenate([f(h_re), f(h_im)], axis=-1)


def _state_from_pairs(h):
    n = h.shape[1]
    half = 2 * SSM_STATE
    f = lambda v: v.transpose(1, 0, 2).reshape(n, SSM_GROUPS, SSM_STATE)
    return f(h[..., :half]), f(h[..., half:])


def _trunk(x, n_seq, n_tok, pos, h0_pairs, kv_past, ops, wts, state_key, tm):
    (norm_mix, norm_ffn, ssm_d, kv_norm, final_norm) = wts["vecs"]
    m = n_seq * n_tok
    u = _prenorm(x, norm_mix[0:1], tm)
    u_pairs, nc = _to_pairs(u, n_seq, n_tok)
    dre, dim = ops["decay"][state_key]
    y_pairs, h_fin = _s5_chunks(u_pairs, ops["toep"], ops["wst"][state_key], ops["wout"], dre, dim,
                                h0_pairs, n_seq, nc)
    y = _from_pairs(y_pairs, n_seq, n_tok, nc)
    h = _s5_tail(x, y, norm_mix[0:1], ssm_d, wts["w_glu"], norm_ffn[0:1], wts["w_up"][0], wts["w_down"][0], tm)
    cos_t, sin_t = _rope_tables(pos)
    if cos_t.shape[0] < tm:
        cos_t, sin_t = (jnp.tile(t, (tm // t.shape[0], 1)) for t in (cos_t, sin_t))
    q, kv, kvb = _qkv(h, norm_mix[1:2], kv_norm, wts["w_q"], wts["w_kv"], cos_t, sin_t, tm)
    if kv_past is None:
        outs, lses = zip(*[_band_attn(q, kvb, n_seq, n_tok, g, dil) for g, (_, dil) in enumerate(BRANCHES)])
    else:
        outs, lses = _sample_attn(q, kv, kv_past, n_seq, n_tok)
    y_out = _attn_tail(h, outs, lses, wts["w_o"], norm_ffn[1:2], wts["w_up"][1], wts["w_down"][1], final_norm,
                       min(tm, 256))
    return y_out, kv, h_fin


def kernel(x_prompt, x_sample, cache_kv_w128, cache_kv_w512, cache_kv_w2048, state_ssm_re, state_ssm_im,
           norm_mix, norm_ffn, ssm_a_re, ssm_a_im, ssm_log_dt, ssm_b_re, ssm_b_im, ssm_c_re, ssm_c_im,
           ssm_d, w_glu, kv_norm, w_kv, w_q, w_o, w_up, w_down, final_norm):
    batch, seq, _ = x_prompt.shape
    dec_batch, dec_seq, _ = x_sample.shape
    assert norm_mix.shape[0] == 2 and ssm_a_re.shape[0] == 1 and w_q.shape[0] == 1
    assert seq % CHUNK == 0 and all(seq % (dil * BAND) == 0 for _, dil in BRANCHES) and dec_seq <= CHUNK

    toep, wstr, wsti, wstr_s, wsti_s, woutr, wouti, dec = _s5_build(
        ssm_a_re[0], ssm_a_im[0], ssm_log_dt[0], ssm_b_re[0], ssm_b_im[0], ssm_c_re[0], ssm_c_im[0], dec_seq)
    dec = dec.reshape(N_PAIRS, 1, 2, 4, SSM_STATE).transpose(3, 0, 1, 2, 4).reshape(4, N_PAIRS, 1, 2 * SSM_STATE)
    ops = {
        "toep": toep,
        "wout": _pair_out_weights(woutr, wouti),
        "wst": {"chunk": _pair_state_weights(wstr, wsti), "sample": _pair_state_weights(wstr_s, wsti_s)},
        "decay": {"chunk": (dec[0], dec[1]), "sample": (dec[2], dec[3])},
    }
    wts = {
        "vecs": (norm_mix, norm_ffn, ssm_d[0:1], kv_norm[None, :], final_norm[None, :]),
        "w_glu": w_glu[0].astype(BF16), "w_q": w_q[0].astype(BF16), "w_kv": w_kv.astype(BF16),
        "w_o": w_o[0].astype(BF16), "w_up": w_up.astype(BF16), "w_down": w_down.astype(BF16),
    }

    pos_p = jnp.arange(seq, dtype=F32)
    pos_s = jnp.tile(PAST_LEN + jnp.arange(dec_seq, dtype=F32), dec_batch)
    zero_state = jnp.zeros((N_PAIRS, batch, 4 * SSM_STATE), F32)
    y_p, kv_p, hf_p = _trunk(x_prompt.reshape(batch * seq, D_MODEL), batch, seq, pos_p, zero_state, None,
                             ops, wts, "chunk", 512)
    caches = [c.reshape(dec_batch, c.shape[1], KV_BRANCH_WIDTH) for c in (cache_kv_w128, cache_kv_w512, cache_kv_w2048)]
    y_s, kv_s, hf_s = _trunk(x_sample.reshape(dec_batch * dec_seq, D_MODEL), dec_batch, dec_seq, pos_s,
                             _state_to_pairs(state_ssm_re[0], state_ssm_im[0]), caches,
                             ops, wts, "sample", dec_batch * dec_seq)

    kv_p = kv_p.reshape(batch, seq, N_BRANCH, 2, KV_HEADS, HEAD_DIM)
    kv_s = kv_s.reshape(dec_batch, dec_seq, N_BRANCH, 2, KV_HEADS, HEAD_DIM)
    kv_p_out = [kv_p[:, seq - min(win, seq):, g] for g, (win, _) in enumerate(BRANCHES)]
    kv_s_out = [kv_s[:, :, g] for g in range(N_BRANCH)]
    re_p, im_p = _state_from_pairs(hf_p)
    re_s, im_s = _state_from_pairs(hf_s)
    return (y_p.reshape(batch, seq, D_MODEL), y_s.reshape(dec_batch, dec_seq, D_MODEL),
            *kv_p_out, *kv_s_out, re_p[None], im_p[None], re_s[None], im_s[None])
```
